```python
import math
import jax
import jax.numpy as jnp
from jax import lax
import numpy as np

D_MODEL = 1024
BATCH = 2
SEQ = 8192
DEPTH = 2
DEC_BATCH = 32
DEC_SEQ = 4
PAST_LEN = 8192
PAGE_SIZE = 128

N_A_LAYERS = DEPTH // 2
N_B_LAYERS = DEPTH - N_A_LAYERS
HEAD_DIM = 64
N_HEADS = D_MODEL // HEAD_DIM
DECAY_LORA = 64
AAA_LORA = 64
GATE_LORA = 160
D_FF = -(-8 * D_MODEL // (3 * 256)) * 256
MOBA_BLOCK = 256
MOBA_TOP_K = 3
Q_CHUNK = 128
ROPE_THETA = 10000.0
RMS_EPS = 1e-6
GN_EPS = 64e-5
NEG_INF = -1e30

kernel_name = 'rwkv7_moba_yoco_step'


def rms_norm(x, g):
    xf = x.astype(jnp.float32)
    y = xf * lax.rsqrt(jnp.mean(xf * xf, axis=-1, keepdims=True) + RMS_EPS)
    return (y * g.astype(jnp.float32)).astype(x.dtype)


def rotary(x, pos):
    half = HEAD_DIM // 2
    inv = jnp.power(ROPE_THETA, -jnp.arange(half, dtype=jnp.float32) * (2.0 / HEAD_DIM))
    ang = pos.astype(jnp.float32)[:, None] * inv[None, :]
    cos = jnp.cos(ang)[None, :, None, :]
    sin = jnp.sin(ang)[None, :, None, :]
    xf = x.astype(jnp.float32)
    x1, x2 = xf[..., :half], xf[..., half:]
    return jnp.concatenate([x1 * cos - x2 * sin, x2 * cos + x1 * sin], axis=-1).astype(x.dtype)


def swiglu(x, wg, wu, wd):
    return (jax.nn.silu(x @ wg) * (x @ wu)) @ wd


def to_heads(t):
    return t.reshape(t.shape[:-1] + (N_HEADS, HEAD_DIM))


def rwkv7_time_mix(xn, x_prev, s0, mu, wr, wk, wv, w0, w1, w2, a0, a1, a2, g1, g2,
                   k_k, k_a, r_k, lnx_w, lnx_b, wo):
    bn, t_len, d = xn.shape
    f32 = jnp.float32
    shifted = jnp.concatenate([x_prev[:, None, :].astype(xn.dtype), xn[:, :-1]], axis=1)
    xx = shifted - xn
    xr, xw, xk, xv, xa, xg = [xn + xx * mu[c] for c in range(6)]
    r = (xr @ wr).astype(f32)
    k = (xk @ wk).astype(f32)
    v = (xv @ wv).astype(f32)
    w_log = -jax.nn.softplus(-(w0 + jnp.tanh(xw @ w1) @ w2).astype(f32)) - 0.5
    decay = jnp.exp(-jnp.exp(w_log))
    a = jax.nn.sigmoid((a0 + (xa @ a1) @ a2).astype(f32))
    g = jax.nn.sigmoid(xg @ g1) @ g2
    kh, rh, vh = to_heads(k), to_heads(r), to_heads(v)
    ah, wh = to_heads(a), to_heads(decay)
    kk = kh * to_heads(k_k.astype(f32))
    kk = kk * lax.rsqrt(jnp.maximum(jnp.sum(kk * kk, axis=-1, keepdims=True), 1e-24))
    kh = kh * (1.0 + (ah - 1.0) * to_heads(k_a.astype(f32)))

    def step(s, inp):
        r_t, w_t, k_t, v_t, kk_t, a_t = inp
        sa = jnp.einsum('bhij,bhj->bhi', s, -kk_t)
        s = (s * w_t[:, :, None, :] + sa[..., None] * (kk_t * a_t)[:, :, None, :]
             + v_t[..., None] * k_t[:, :, None, :])
        return s, jnp.einsum('bhij,bhj->bhi', s, r_t)

    tm = lambda z: jnp.swapaxes(z, 0, 1)
    s_final, o = lax.scan(step, s0.astype(f32), (tm(rh), tm(wh), tm(kh), tm(vh), tm(kk), tm(ah)))
    o = tm(o)
    mean = jnp.mean(o, axis=-1, keepdims=True)
    var = jnp.mean(jnp.square(o - mean), axis=-1, keepdims=True)
    o = ((o - mean) * lax.rsqrt(var + GN_EPS)).reshape(bn, t_len, d) * lnx_w.astype(f32) + lnx_b.astype(f32)
    bonus = jnp.sum(rh * kh * r_k.astype(f32), axis=-1, keepdims=True) * vh
    o = o + bonus.reshape(bn, t_len, d)
    out = (o.astype(xn.dtype) * g) @ wo
    return out, s_final.astype(xn.dtype), xn[:, -1]


def shared_kv(h, pos, kv_norm, kv_wk, kv_wv):
    hn = rms_norm(h, kv_norm)
    k = rotary(to_heads(hn @ kv_wk), pos)
    v = to_heads(hn @ kv_wv)
    return k, v


def moba_prompt(q, k, v):
    bn, t_len, h, dh = q.shape
    scale = HEAD_DIM ** -0.5
    nb = -(-t_len // MOBA_BLOCK)
    pad = nb * MOBA_BLOCK - t_len
    padw = ((0, 0), (0, pad), (0, 0), (0, 0))
    kb = jnp.pad(k, padw).reshape(bn, nb, MOBA_BLOCK, h, dh).transpose(0, 3, 1, 2, 4)
    vb = jnp.pad(v, padw).reshape(bn, nb, MOBA_BLOCK, h, dh).transpose(0, 3, 1, 2, 4)
    means = jnp.mean(kb.astype(jnp.float32), axis=3)
    k_sel = min(MOBA_TOP_K, nb - 1)
    n_chunks = t_len // Q_CHUNK
    qc = jnp.swapaxes(q.reshape(bn, n_chunks, Q_CHUNK, h, dh), 0, 1)
    bi = jnp.arange(bn)[:, None, None, None]
    hi = jnp.arange(h)[None, None, :, None]

    def chunk(args):
        c, qi = args
        qpos = c * Q_CHUNK + jnp.arange(Q_CHUNK)
        blk = (c * Q_CHUNK) // MOBA_BLOCK
        k_own = lax.dynamic_index_in_dim(kb, blk, axis=2, keepdims=False)
        v_own = lax.dynamic_index_in_dim(vb, blk, axis=2, keepdims=False)
        kpos = blk * MOBA_BLOCK + jnp.arange(MOBA_BLOCK)
        s_own = jnp.einsum('bqhd,bhkd->bqhk', qi, k_own, preferred_element_type=jnp.float32) * scale
        s_own = jnp.where((kpos[None, :] <= qpos[:, None])[None, :, None, :], s_own, NEG_INF)
        if k_sel == 0:
            p_own = jax.nn.softmax(s_own, axis=-1).astype(v.dtype)
            return jnp.einsum('bqhk,bhkd->bqhd', p_own, v_own)
        gate = jnp.einsum('bqhd,bhnd->bqhn', qi.astype(jnp.float32), means)
        gate = jnp.where(jnp.arange(nb)[None, None, None, :] < blk, gate, NEG_INF)
        _, idx = lax.top_k(gate, k_sel)
        valid = idx < blk
        k_g = kb[bi, hi, idx]
        v_g = vb[bi, hi, idx]
        s_sel = jnp.einsum('bqhd,bqhnkd->bqhnk', qi, k_g, preferred_element_type=jnp.float32) * scale
        s_sel = jnp.where(valid[..., None], s_sel, NEG_INF)
        n_sel = k_sel * MOBA_BLOCK
        s = jnp.concatenate([s_sel.reshape(bn, Q_CHUNK, h, n_sel), s_own], axis=-1)
        p = jax.nn.softmax(s, axis=-1).astype(v.dtype)
        p_sel = p[..., :n_sel].reshape(bn, Q_CHUNK, h, k_sel, MOBA_BLOCK)
        return (jnp.einsum('bqhnk,bqhnkd->bqhd', p_sel, v_g)
                + jnp.einsum('bqhk,bhkd->bqhd', p[..., n_sel:], v_own))

    out = lax.map(chunk, (jnp.arange(n_chunks, dtype=jnp.int32), qc))
    return jnp.swapaxes(out, 0, 1).reshape(bn, t_len, h, dh)


def moba_sample(q, k_new, v_new, cache_k, cache_v, page_table):
    db, s_len, h, dh = q.shape
    scale = HEAD_DIM ** -0.5
    n_pages = page_table.shape[1]
    past = n_pages * PAGE_SIZE
    ppb = MOBA_BLOCK // PAGE_SIZE
    blk = past // MOBA_BLOCK
    r0 = past - blk * MOBA_BLOCK
    assert r0 + s_len <= MOBA_BLOCK
    k_sel = min(MOBA_TOP_K, blk)
    s_new = jnp.einsum('bqhd,bkhd->bqhk', q, k_new, preferred_element_type=jnp.float32) * scale
    causal = jnp.tril(jnp.ones((s_len, s_len), dtype=bool))
    s_new = jnp.where(causal[None, :, None, :], s_new, NEG_INF)
    s_parts = []
    if k_sel > 0:
        k_full = cache_k[page_table[:, :blk * ppb]].reshape(db, blk, MOBA_BLOCK, h, dh)
        means = jnp.mean(k_full.astype(jnp.float32), axis=2)
        gate = jnp.einsum('bqhd,bnhd->bqhn', q.astype(jnp.float32), means)
        _, idx = lax.top_k(gate, k_sel)
        logical = idx[..., None] * ppb + jnp.arange(ppb)
        phys = page_table[jnp.arange(db)[:, None, None, None, None], logical]
        hi = jnp.arange(h)[None, None, :, None, None]
        k_g = cache_k[phys, :, hi, :].reshape(db, s_len, h, k_sel, MOBA_BLOCK, dh)
        v_g = cache_v[phys, :, hi, :].reshape(db, s_len, h, k_sel, MOBA_BLOCK, dh)
        s_sel = jnp.einsum('bqhd,bqhnkd->bqhnk', q, k_g, preferred_element_type=jnp.float32) * scale
        s_parts.append(s_sel.reshape(db, s_len, h, k_sel * MOBA_BLOCK))
    if r0 > 0:
        own_pages = page_table[:, blk * ppb:]
        k_op = cache_k[own_pages].reshape(db, r0, h, dh)
        v_op = cache_v[own_pages].reshape(db, r0, h, dh)
        s_parts.append(jnp.einsum('bqhd,bkhd->bqhk', q, k_op, preferred_element_type=jnp.float32) * scale)
    s_parts.append(s_new)
    p = jax.nn.softmax(jnp.concatenate(s_parts, axis=-1), axis=-1).astype(v_new.dtype)
    n_new = p.shape[-1] - s_len
    o = jnp.einsum('bqhk,bkhd->bqhd', p[..., n_new:], v_new)
    off = 0
    if k_sel > 0:
        n_sel = k_sel * MOBA_BLOCK
        p_sel = p[..., :n_sel].reshape(db, s_len, h, k_sel, MOBA_BLOCK)
        o = o + jnp.einsum('bqhnk,bqhnkd->bqhd', p_sel, v_g)
        off = n_sel
    if r0 > 0:
        o = o + jnp.einsum('bqhk,bkhd->bqhd', p[..., off:off + r0], v_op)
    return o


def setup_inputs(seed: int = 0) -> dict:
    key = jax.random.key(seed)
    ks = list(jax.random.split(key, 40))
    f32 = jnp.float32
    D, H, N, F = D_MODEL, N_HEADS, HEAD_DIM, D_FF
    NA, NB = N_A_LAYERS, N_B_LAYERS

    def nrm(shape, scale):
        return jax.random.normal(ks.pop(), shape, f32) * scale

    n_pages = PAST_LEN // PAGE_SIZE
    n_phys = (5 * DEC_BATCH * n_pages + 3) // 4
    x_prompt = nrm((BATCH, SEQ, D), 1.0)
    x_sample = nrm((DEC_BATCH, DEC_SEQ, D), 1.0)
    state_wkv = nrm((NA, DEC_BATCH, H, N, N), 0.5)
    state_shift = nrm((NA, DEC_BATCH, D), 1.0)
    cache_k = nrm((n_phys, PAGE_SIZE, H, N), 1.0)
    cache_v = nrm((n_phys, PAGE_SIZE, H, N), 1.0)
    perm = jax.random.permutation(ks.pop(), n_phys)
    page_table = perm[:DEC_BATCH * n_pages].reshape(DEC_BATCH, n_pages).astype(jnp.int32)
    ratio = jnp.arange(D, dtype=f32) / (D - 1)
    return {
        'x_prompt': x_prompt, 'x_sample': x_sample,
        'state_wkv': state_wkv, 'state_shift': state_shift,
        'cache_k': cache_k, 'cache_v': cache_v, 'page_table': page_table,
        'rwkv_norm': 1.0 + nrm((NA, D), 0.02),
        'rwkv_mu': jax.random.uniform(ks.pop(), (NA, 6, D), f32),
        'rwkv_wr': nrm((NA, D, D), D ** -0.5),
        'rwkv_wk': nrm((NA, D, D), D ** -0.5),
        'rwkv_wv': nrm((NA, D, D), D ** -0.5),
        'rwkv_w0': (-6.5 + 5.0 * ratio ** 0.85) + nrm((NA, D), 0.1),
        'rwkv_w1': nrm((NA, D, DECAY_LORA), D ** -0.5),
        'rwkv_w2': nrm((NA, DECAY_LORA, D), 0.5 * DECAY_LORA ** -0.5),
        'rwkv_a0': nrm((NA, D), 0.1),
        'rwkv_a1': nrm((NA, D, AAA_LORA), D ** -0.5),
        'rwkv_a2': nrm((NA, AAA_LORA, D), 0.5 * AAA_LORA ** -0.5),
        'rwkv_g1': nrm((NA, D, GATE_LORA), D ** -0.5),
        'rwkv_g2': nrm((NA, GATE_LORA, D), GATE_LORA ** -0.5),
        'rwkv_kk': 0.85 + nrm((NA, D), 0.05),
        'rwkv_ka': 1.0 + nrm((NA, D), 0.05),
        'rwkv_rk': nrm((NA, H, N), 0.1),
        'rwkv_lnx_w': 1.0 + nrm((NA, D), 0.02),
        'rwkv_lnx_b': nrm((NA, D), 0.02),
        'rwkv_wo': nrm((NA, D, D), D ** -0.5),
        'kv_norm': 1.0 + nrm((D,), 0.02),
        'kv_wk': nrm((D, D), D ** -0.5),
        'kv_wv': nrm((D, D), D ** -0.5),
        'attn_norm': 1.0 + nrm((NB, D), 0.02),
        'attn_wq': nrm((NB, D, D), D ** -0.5),
        'attn_wo': nrm((NB, D, D), D ** -0.5),
        'ffn_norm': 1.0 + nrm((DEPTH, D), 0.02),
        'ffn_wg': nrm((DEPTH, D, F), D ** -0.5),
        'ffn_wu': nrm((DEPTH, D, F), D ** -0.5),
        'ffn_wd': nrm((DEPTH, F, D), F ** -0.5),
        'final_norm': 1.0 + nrm((D,), 0.02),
    }


def reference(x_prompt, x_sample, state_wkv, state_shift, cache_k, cache_v, page_table,
              rwkv_norm, rwkv_mu, rwkv_wr, rwkv_wk, rwkv_wv, rwkv_w0, rwkv_w1, rwkv_w2,
              rwkv_a0, rwkv_a1, rwkv_a2, rwkv_g1, rwkv_g2, rwkv_kk, rwkv_ka, rwkv_rk,
              rwkv_lnx_w, rwkv_lnx_b, rwkv_wo, kv_norm, kv_wk, kv_wv,
              attn_norm, attn_wq, attn_wo, ffn_norm, ffn_wg, ffn_wu, ffn_wd, final_norm):

    def trunk(h, pos, shift0, wkv0, attend):
        bn, t_len, d = h.shape
        new_wkv, new_shift = [], []
        k_sh, v_sh = None, None
        for i in range(DEPTH):
            if i < N_A_LAYERS:
                xn = rms_norm(h, rwkv_norm[i])
                o, s_fin, last = rwkv7_time_mix(
                    xn, shift0[i], wkv0[i], rwkv_mu[i], rwkv_wr[i], rwkv_wk[i], rwkv_wv[i],
                    rwkv_w0[i], rwkv_w1[i], rwkv_w2[i], rwkv_a0[i], rwkv_a1[i], rwkv_a2[i],
                    rwkv_g1[i], rwkv_g2[i], rwkv_kk[i], rwkv_ka[i], rwkv_rk[i],
                    rwkv_lnx_w[i], rwkv_lnx_b[i], rwkv_wo[i])
                new_wkv.append(s_fin)
                new_shift.append(last)
            else:
                j = i - N_A_LAYERS
                xn = rms_norm(h, attn_norm[j])
                q = rotary(to_heads(xn @ attn_wq[j]), pos)
                o = attend(q, k_sh, v_sh).reshape(bn, t_len, d) @ attn_wo[j]
            h = h + o
            h = h + swiglu(rms_norm(h, ffn_norm[i]), ffn_wg[i], ffn_wu[i], ffn_wd[i])
            if i == N_A_LAYERS - 1:
                k_sh, v_sh = shared_kv(h, pos, kv_norm, kv_wk, kv_wv)
        return rms_norm(h, final_norm), jnp.stack(new_wkv), jnp.stack(new_shift), k_sh, v_sh

    bp, seq, _ = x_prompt.shape
    wkv_zero = jnp.zeros((N_A_LAYERS, bp, N_HEADS, HEAD_DIM, HEAD_DIM), x_prompt.dtype)
    shift_zero = jnp.zeros((N_A_LAYERS, bp, D_MODEL), x_prompt.dtype)
    y_prompt, wkv_p, shift_p, k_p, v_p = trunk(
        x_prompt, jnp.arange(seq, dtype=jnp.int32), shift_zero, wkv_zero, moba_prompt)
    k_pages_p = k_p.reshape(bp, seq // PAGE_SIZE, PAGE_SIZE, N_HEADS, HEAD_DIM)
    v_pages_p = v_p.reshape(bp, seq // PAGE_SIZE, PAGE_SIZE, N_HEADS, HEAD_DIM)

    past = page_table.shape[1] * PAGE_SIZE
    ds = x_sample.shape[1]
    pos_s = past + jnp.arange(ds, dtype=jnp.int32)
    attend_s = lambda q, k, v: moba_sample(q, k, v, cache_k, cache_v, page_table)
    y_sample, wkv_s, shift_s, k_s, v_s = trunk(x_sample, pos_s, state_shift, state_wkv, attend_s)
    return (y_prompt, y_sample, wkv_p, shift_p, k_pages_p, v_pages_p, wkv_s, shift_s, k_s, v_s)
```

```python
import functools

import jax
import jax.numpy as jnp
from jax import lax
from jax.experimental import pallas as pl
from jax.experimental.pallas import tpu as pltpu

F32 = jnp.float32
BF16 = jnp.bfloat16

HEAD_DIM = 64
LANES = 128
MOBA_BLOCK = 256
MOBA_TOP_K = 3
PAGE_SIZE = 128
ROPE_THETA = 10000.0
RMS_EPS = 1e-6
GN_EPS = 64e-5
NEG_INF = -1e30
SCAN_CHUNK = 64
VMEM_LIMIT = 56 * 1024 * 1024


def _dot_nn(a, b):
    return jnp.dot(a, b, preferred_element_type=F32)


def _dot_nt(a, b):
    return lax.dot_general(a, b, (((1,), (1,)), ((), ())), preferred_element_type=F32)


def _dot_tn(a, b):
    return lax.dot_general(a, b, (((0,), (0,)), ((), ())), preferred_element_type=F32)


def _split(x):
    hi = x.astype(BF16)
    lo = (x - hi.astype(F32)).astype(BF16)
    return hi, lo


def _mm3(dot, a, b):
    ah, al = _split(a)
    bh, bl = _split(b)
    return dot(ah, bh) + (dot(ah, bl) + dot(al, bh))


def _mm2l(dot, a, b_bf16):
    ah, al = _split(a)
    return dot(ah, b_bf16) + dot(al, b_bf16)


def _bdot(a, b_bf16):
    return jnp.dot(a.astype(BF16), b_bf16, preferred_element_type=F32)


def _rms(x, g):
    return x * lax.rsqrt(jnp.mean(x * x, axis=-1, keepdims=True) + RMS_EPS) * g


def _head_sum(x, e_ref, et_ref):
    s = _mm2l(_dot_nn, x, e_ref[...])
    return _mm2l(_dot_nn, s, et_ref[...])


def _top_k_select(gate, valid, k_sel):
    lane = lax.broadcasted_iota(jnp.int32, gate.shape, 1).astype(F32)
    cur = jnp.where(valid, gate, NEG_INF)
    sel = jnp.zeros(gate.shape, dtype=jnp.bool_)
    for _ in range(k_sel):
        mx = jnp.max(cur, axis=1, keepdims=True)
        cand = jnp.where(cur == mx, lane, float(LANES))
        idx = jnp.min(cand, axis=1, keepdims=True)
        pick = lane == idx
        sel = jnp.logical_or(sel, pick)
        cur = jnp.where(pick, -jnp.inf, cur)
    return jnp.logical_and(sel, valid)


def _rope(x, cos128, sin128):
    d = x.shape[-1]
    reps = d // LANES
    cosf = jnp.concatenate([cos128] * reps, axis=1)
    sinf = jnp.concatenate([sin128] * reps, axis=1)
    lane = lax.broadcasted_iota(jnp.int32, x.shape, 1)
    first_half = (lane % HEAD_DIM) < (HEAD_DIM // 2)
    rot = jnp.where(first_half,
                    pltpu.roll(x, d - HEAD_DIM // 2, 1),
                    pltpu.roll(x, HEAD_DIM // 2, 1))
    return x * cosf + rot * sinf


def _rwkv_pre_kernel(x_ref, xp_ref, sh_ref, nw_ref, mu_ref, wr_ref, wk_ref, wv_ref,
                     w0_ref, w1_ref, w2_ref, a0_ref, a1_ref, a2_ref, g1_ref, g2_ref,
                     kkw_ref, kaw_ref, rkw_ref, e_ref, et_ref,
                     r_out, lw_out, k_out, v_out, kk_out, kka_out, g_out, bonus_out, last_out,
                     *, seq_len, rows_mode):
    tm = x_ref.shape[0]
    nw = nw_ref[...]
    xn = _rms(x_ref[...], nw)
    rolled = pltpu.roll(xn, 1, 0)
    row = lax.broadcasted_iota(jnp.int32, xn.shape, 0)
    if rows_mode:
        shifted = jnp.where(row % seq_len == 0, sh_ref[...], rolled)
        last_out[...] = xn
    else:
        tiles_per_seq = seq_len // tm
        first = (pl.program_id(0) % tiles_per_seq) == 0
        prev = jnp.where(first, sh_ref[0], _rms(xp_ref[7:8, :], nw))
        shifted = jnp.where(row == 0, prev, rolled)
        last_out[0] = xn[tm - 1:tm, :]
    xx = shifted - xn
    xr, xw, xk, xv, xa, xg = [xn + xx * mu_ref[c:c + 1, :] for c in range(6)]

    r = _bdot(xr, wr_ref[...])
    k = _bdot(xk, wk_ref[...])
    v = _bdot(xv, wv_ref[...])
    z = w0_ref[...] + _mm3(_dot_nn, jnp.tanh(_mm3(_dot_nn, xw, w1_ref[...])), w2_ref[...])
    softplus_neg = jnp.maximum(-z, 0.0) + jnp.log(1.0 + jnp.exp(-jnp.abs(z)))
    lw = -jnp.exp(-softplus_neg - 0.5)
    a = jax.nn.sigmoid(a0_ref[...] + _mm3(_dot_nn, _mm3(_dot_nn, xa, a1_ref[...]), a2_ref[...]))
    g = _mm3(_dot_nn, jax.nn.sigmoid(_mm3(_dot_nn, xg, g1_ref[...])), g2_ref[...])

    kk = k * kkw_ref[...]
    ss = _head_sum(kk * kk, e_ref, et_ref)
    kk = kk * lax.rsqrt(jnp.maximum(ss, 1e-24))
    k2 = k * (1.0 + (a - 1.0) * kaw_ref[...])
    bonus = _head_sum(r * k2 * rkw_ref[...], e_ref, et_ref) * v

    r_out[...] = r
    lw_out[...] = lw
    k_out[...] = k2
    v_out[...] = v
    kk_out[...] = kk
    kka_out[...] = kk * a
    g_out[...] = g
    bonus_out[...] = bonus


def _rwkv_pre(x2, shift_arg, w, *, seq_len, rows_mode, tm):
    n, d = x2.shape
    n_tiles = n // tm
    full = lambda shape: pl.BlockSpec(shape, lambda i: (0,) * len(shape))
    tok = pl.BlockSpec((tm, d), lambda i: (i, 0))
    if rows_mode:
        sh_spec = pl.BlockSpec((tm, d), lambda i: (i, 0))
        last_shape = jax.ShapeDtypeStruct((n, d), F32)
        last_spec = tok
    else:
        tiles_per_seq = seq_len // tm
        sh_spec = pl.BlockSpec((1, 1, d), lambda i: (i // tiles_per_seq, 0, 0))
        last_shape = jax.ShapeDtypeStruct((n_tiles, 1, d), F32)
        last_spec = pl.BlockSpec((1, 1, d), lambda i: (i, 0, 0))
    xp_spec = pl.BlockSpec((8, d), lambda i: (jnp.maximum(i * (tm // 8) - 1, 0), 0))
    weights = [w['norm'], w['mu'], w['wr'], w['wk'], w['wv'], w['w0'], w['w1'], w['w2'],
               w['a0'], w['a1'], w['a2'], w['g1'], w['g2'], w['kk'], w['ka'], w['rk'],
               w['e'], w['et']]
    tok_shape = jax.ShapeDtypeStruct((n, d), F32)
    return pl.pallas_call(
        functools.partial(_rwkv_pre_kernel, seq_len=seq_len, rows_mode=rows_mode),
        grid=(n_tiles,),
        in_specs=[tok, xp_spec, sh_spec] + [full(a.shape) for a in weights],
        out_specs=[tok] * 8 + [last_spec],
        out_shape=[tok_shape] * 8 + [last_shape],
        compiler_params=pltpu.CompilerParams(dimension_semantics=("arbitrary",),
                                             vmem_limit_bytes=VMEM_LIMIT),
        name="rwkv_pre",
    )(x2, x2, shift_arg, *weights)


def _wkv_scan_kernel(r_ref, lw_ref, k_ref, v_ref, kk_ref, kka_ref, s0_ref,
                     o_ref, sfin_ref, s_scr):
    c = pl.program_id(1)
    n_chunks = pl.num_programs(1)
    cs = r_ref.shape[1]
    d = r_ref.shape[2]
    n_pairs = d // LANES
    c2 = 2 * cs

    @pl.when(c == 0)
    def _():
        s_scr[...] = s0_ref[0]

    lw = lw_ref[0]
    ti = lax.broadcasted_iota(jnp.int32, (cs, cs), 0)
    tj = lax.broadcasted_iota(jnp.int32, (cs, cs), 1)
    tri = (ti >= tj).astype(BF16)
    l1 = lw.astype(BF16)
    l2 = (lw - l1.astype(F32)).astype(BF16)
    l3 = (lw - l1.astype(F32) - l2.astype(F32)).astype(BF16)
    cum = _dot_nn(tri, l1) + (_dot_nn(tri, l2) + _dot_nn(tri, l3))
    cend = cum[cs - 1:cs, :]
    p_inc = jnp.exp(cum)
    p_prev = jnp.exp(cum - lw)
    p_inv = jnp.exp(-cum)
    p_tail = jnp.exp(cend - cum)
    p_end = jnp.exp(cend)

    kk = kk_ref[0]
    kka = kka_ref[0]
    kx = k_ref[0]
    a_t = -kk * p_prev
    b_t = kka * p_inv
    k_t = kx * p_inv
    r_t = r_ref[0] * p_inc
    b_p = kka * p_tail
    k_p = kx * p_tail
    vx = v_ref[0]

    lane = lax.broadcasted_iota(jnp.int32, (1, LANES), 1)
    m0 = (lane < HEAD_DIM).astype(F32)
    m1 = 1.0 - m0
    ri = lax.broadcasted_iota(jnp.int32, (c2, c2), 0)
    ci = lax.broadcasted_iota(jnp.int32, (c2, c2), 1)
    same = (ri // cs) == (ci // cs)
    strict = jnp.logical_and(same, (ci % cs) < (ri % cs))
    incl = jnp.logical_and(same, (ci % cs) <= (ri % cs))
    eye = (ri == ci).astype(F32)

    def stack(x):
        return jnp.concatenate([x * m0, x * m1], axis=0)

    n_sq = cs.bit_length() - 2

    for p in range(n_pairs):
        sl = slice(p * LANES, (p + 1) * LANES)
        a2, r2 = stack(a_t[:, sl]), stack(r_t[:, sl])
        b2, k2 = stack(b_t[:, sl]), stack(k_t[:, sl])
        v2 = stack(vx[:, sl])
        bp2, kp2 = stack(b_p[:, sl]), stack(k_p[:, sl])
        ar = jnp.concatenate([a2, r2], axis=0)
        lm = _mm3(_dot_nt, ar, jnp.concatenate([b2, k2], axis=0))
        l_ab = jnp.where(strict, lm[:c2, :c2], 0.0)
        l_ak = jnp.where(strict, lm[:c2, c2:], 0.0)
        m_rb = jnp.where(incl, lm[c2:, :c2], 0.0)
        m_rk = jnp.where(incl, lm[c2:, c2:], 0.0)
        s = s_scr[p]
        ars = _mm3(_dot_nt, ar, s)
        t_inv = eye + l_ab
        l_pow = l_ab
        for _ in range(n_sq):
            l_pow = _mm3(_dot_nn, l_pow, l_pow)
            t_inv = t_inv + _mm3(_dot_nn, t_inv, l_pow)
        u2 = _mm3(_dot_nn, t_inv, ars[:c2] + _mm3(_dot_nn, l_ak, v2))
        uv = jnp.concatenate([u2, v2], axis=0)
        o2 = ars[c2:] + _mm3(_dot_nn, jnp.concatenate([m_rb, m_rk], axis=1), uv)
        o_ref[0, :, sl] = o2[:cs] + o2[cs:]
        s_new = s * p_end[:, sl] + _mm3(_dot_tn, uv, jnp.concatenate([bp2, kp2], axis=0))
        s_scr[p] = s_new

    @pl.when(c == n_chunks - 1)
    def _():
        sfin_ref[0] = s_scr[...]


def _wkv_scan(r, lw, k, v, kk, kka, s0_bd):
    b, t, d = r.shape
    cs = SCAN_CHUNK
    n_pairs = d // LANES
    tok = pl.BlockSpec((1, cs, d), lambda i, c: (i, c, 0))
    st = pl.BlockSpec((1, n_pairs, LANES, LANES), lambda i, c: (i, 0, 0, 0))
    return pl.pallas_call(
        _wkv_scan_kernel,
        grid=(b, t // cs),
        in_specs=[tok] * 6 + [st],
        out_specs=[tok, st],
        out_shape=[jax.ShapeDtypeStruct((b, t, d), F32),
                   jax.ShapeDtypeStruct((b, n_pairs, LANES, LANES), F32)],
        scratch_shapes=[pltpu.VMEM((n_pairs, LANES, LANES), F32)],
        compiler_params=pltpu.CompilerParams(dimension_semantics=("arbitrary", "arbitrary"),
                                             vmem_limit_bytes=VMEM_LIMIT),
        name="wkv_scan",
    )(r, lw, k, v, kk, kka, s0_bd)


def _rwkv_post_kernel(o_ref, bonus_ref, g_ref, x_ref, lnw_ref, lnb_ref, wo_ref, e_ref, et_ref, h_ref):
    o = o_ref[...]
    inv_n = 1.0 / HEAD_DIM
    mean = _head_sum(o, e_ref, et_ref) * inv_n
    dlt = o - mean
    var = _head_sum(dlt * dlt, e_ref, et_ref) * inv_n
    on = dlt * lax.rsqrt(var + GN_EPS) * lnw_ref[...] + lnb_ref[...] + bonus_ref[...]
    h_ref[...] = x_ref[...] + _bdot(on * g_ref[...], wo_ref[...])


def _rwkv_post(o2, bonus, g, x2, w, *, tm):
    n, d = x2.shape
    tok = pl.BlockSpec((tm, d), lambda i: (i, 0))
    full = lambda shape: pl.BlockSpec(shape, lambda i: (0,) * len(shape))
    weights = [w['lnx_w'], w['lnx_b'], w['wo'], w['e'], w['et']]
    return pl.pallas_call(
        _rwkv_post_kernel,
        grid=(n // tm,),
        in_specs=[tok] * 4 + [full(a.shape) for a in weights],
        out_specs=tok,
        out_shape=jax.ShapeDtypeStruct((n, d), F32),
        compiler_params=pltpu.CompilerParams(dimension_semantics=("arbitrary",),
                                             vmem_limit_bytes=VMEM_LIMIT),
        name="rwkv_post",
    )(o2, bonus, g, x2, *weights)


def _ffn_kernel(h_ref, nw_ref, wg_ref, wu_ref, wd_ref, fw_ref, y_ref, xn_scr, acc_scr, *, final_norm):
    j = pl.program_id(1)

    @pl.when(j == 0)
    def _():
        xn_scr[...] = _rms(h_ref[...], nw_ref[...]).astype(BF16)
        acc_scr[...] = jnp.zeros_like(acc_scr)

    xn = xn_scr[...]
    gate = jnp.dot(xn, wg_ref[...], preferred_element_type=F32)
    up = jnp.dot(xn, wu_ref[...], preferred_element_type=F32)
    hid = gate * jax.nn.sigmoid(gate) * up
    acc_scr[...] += jnp.dot(hid.astype(BF16), wd_ref[...], preferred_element_type=F32)

    @pl.when(j == pl.num_programs(1) - 1)
    def _():
        y = h_ref[...] + acc_scr[...]
        if final_norm:
            y = _rms(y, fw_ref[...])
        y_ref[...] = y


def _ffn(h2, nw, wg, wu, wd, fw, *, final_norm, tm, tf):
    n, d = h2.shape
    f = wg.shape[1]
    tok = pl.BlockSpec((tm, d), lambda i, j: (i, 0))
    vec = pl.BlockSpec((1, d), lambda i, j: (0, 0))
    return pl.pallas_call(
        functools.partial(_ffn_kernel, final_norm=final_norm),
        grid=(n // tm, f // tf),
        in_specs=[tok, vec,
                  pl.BlockSpec((d, tf), lambda i, j: (0, j)),
                  pl.BlockSpec((d, tf), lambda i, j: (0, j)),
                  pl.BlockSpec((tf, d), lambda i, j: (j, 0)),
                  vec],
        out_specs=tok,
        out_shape=jax.ShapeDtypeStruct((n, d), F32),
        scratch_shapes=[pltpu.VMEM((tm, d), BF16), pltpu.VMEM((tm, d), F32)],
        compiler_params=pltpu.CompilerParams(dimension_semantics=("arbitrary", "arbitrary"),
                                             vmem_limit_bytes=VMEM_LIMIT),
        name="ffn",
    )(h2, nw, wg, wu, wd, fw)


def _qkv_kernel(h_ref, cos_ref, sin_ref, kvn_ref, atn_ref, wk_ref, wv_ref, wq_ref,
                q_ref, k_ref, v_ref):
    h = h_ref[...]
    cos128 = cos_ref[...]
    sin128 = sin_ref[...]
    hn = _rms(h, kvn_ref[...])
    k_ref[...] = _rope(_bdot(hn, wk_ref[...]), cos128, sin128)
    v_ref[...] = _bdot(hn, wv_ref[...])
    xa = _rms(h, atn_ref[...])
    q_ref[...] = _rope(_bdot(xa, wq_ref[...]), cos128, sin128)


def _qkv(h2, cos_tab, sin_tab, kvn, atn, wk, wv, wq, *, tm):
    n, d = h2.shape
    pos_tiles = cos_tab.shape[0] // tm
    tok = pl.BlockSpec((tm, d), lambda i: (i, 0))
    tab = pl.BlockSpec((tm, LANES), lambda i: (i % pos_tiles, 0))
    vec = pl.BlockSpec((1, d), lambda i: (0, 0))
    mat = pl.BlockSpec((d, d), lambda i: (0, 0))
    shp = jax.ShapeDtypeStruct((n, d), F32)
    return pl.pallas_call(
        _qkv_kernel,
        grid=(n // tm,),
        in_specs=[tok, tab, tab, vec, vec, mat, mat, mat],
        out_specs=[tok, tok, tok],
        out_shape=[shp, shp, shp],
        compiler_params=pltpu.CompilerParams(dimension_semantics=("arbitrary",),
                                             vmem_limit_bytes=VMEM_LIMIT),
        name="qkv_proj",
    )(h2, cos_tab, sin_tab, kvn, atn, wk, wv, wq)


def _proj_res_kernel(a_ref, h_ref, w_ref, y_ref):
    y_ref[...] = h_ref[...] + _bdot(a_ref[...], w_ref[...])


def _proj_res(a2, h2, w, *, tm):
    n, d = h2.shape
    tok = pl.BlockSpec((tm, d), lambda i: (i, 0))
    return pl.pallas_call(
        _proj_res_kernel,
        grid=(n // tm,),
        in_specs=[tok, tok, pl.BlockSpec((d, d), lambda i: (0, 0))],
        out_specs=tok,
        out_shape=jax.ShapeDtypeStruct((n, d), F32),
        compiler_params=pltpu.CompilerParams(dimension_semantics=("arbitrary",),
                                             vmem_limit_bytes=VMEM_LIMIT),
        name="proj_res",
    )(a2, h2, w)


def _moba_prompt_kernel(q_ref, k_ref, v_ref, o_ref, kb_scr, vt_scr, mean_scr, sel_scr, *, k_sel):
    i = pl.program_id(2)
    blk = MOBA_BLOCK
    nb = k_ref.shape[1] // blk

    @pl.when(i == 0)
    def _():
        mean_scr[...] = jnp.zeros_like(mean_scr)

        def prep(n, carry):
            rows = pl.ds(pl.multiple_of(n * blk, blk), blk)
            kblk = k_ref[0, rows, :]
            kb_scr[rows, :] = kblk.astype(BF16)
            mean_scr[pl.ds(n, 1), :] = jnp.mean(kblk, axis=0, keepdims=True)
            vt_scr[n] = v_ref[0, rows, :].T.astype(BF16)
            return carry

        lax.fori_loop(0, nb, prep, 0)

    q = q_ref[0]
    lane = lax.broadcasted_iota(jnp.int32, (1, LANES), 1)
    head_masks = [(lane < HEAD_DIM).astype(F32), (lane >= HEAD_DIM).astype(F32)]
    blk_lane = lax.broadcasted_iota(jnp.int32, (blk, LANES), 1)
    valid = blk_lane < i
    means = mean_scr[...]
    scale = HEAD_DIM ** -0.5

    qts = []
    for e in range(2):
        qz = q * head_masks[e]
        gate = _mm3(_dot_nt, qz, means)
        sel = _top_k_select(gate, valid, k_sel)
        sel_scr[e] = jnp.where(sel, 0.0, NEG_INF).T
        qts.append((qz * scale).T.astype(BF16))

    kr = lax.broadcasted_iota(jnp.int32, (blk, blk), 0)
    qc = lax.broadcasted_iota(jnp.int32, (blk, blk), 1)
    causal = kr <= qc
    own = pl.ds(pl.multiple_of(i * blk, blk), blk)
    k_own = kb_scr[own, :]
    vt_own = vt_scr[i]
    carry = []
    for e in range(2):
        st = jnp.where(causal, _dot_nn(k_own, qts[e]), NEG_INF)
        m = jnp.max(st, axis=0, keepdims=True)
        pexp = jnp.exp(st - m)
        l = jnp.sum(pexp, axis=0, keepdims=True)
        acc = _dot_nn(vt_own[e * HEAD_DIM:(e + 1) * HEAD_DIM, :], pexp.astype(BF16))
        carry += [m, l, acc]

    def body(n, carry):
        rows = pl.ds(pl.multiple_of(n * blk, blk), blk)
        kn = kb_scr[rows, :]
        vtn = vt_scr[n]
        out = []
        for e in range(2):
            m, l, acc = carry[3 * e:3 * e + 3]
            st = _dot_nn(kn, qts[e]) + sel_scr[e, pl.ds(n, 1), :]
            m_new = jnp.maximum(m, jnp.max(st, axis=0, keepdims=True))
            alpha = jnp.exp(m - m_new)
            pexp = jnp.exp(st - m_new)
            l = alpha * l + jnp.sum(pexp, axis=0, keepdims=True)
            acc = alpha * acc + _dot_nn(vtn[e * HEAD_DIM:(e + 1) * HEAD_DIM, :], pexp.astype(BF16))
            out += [m_new, l, acc]
        return tuple(out)

    carry = lax.fori_loop(0, i, body, tuple(carry))
    outs = [carry[3 * e + 2] / carry[3 * e + 1] for e in range(2)]
    o_ref[0] = jnp.concatenate(outs, axis=0).T


def _moba_prompt(q, k, v):
    b, t, d = q.shape
    blk = MOBA_BLOCK
    nb = t // blk
    k_sel = min(MOBA_TOP_K, nb - 1)
    kernel = functools.partial(_moba_prompt_kernel, k_sel=k_sel)
    return pl.pallas_call(
        kernel,
        grid=(b, d // LANES, nb),
        in_specs=[pl.BlockSpec((1, blk, LANES), lambda bi, hp, i: (bi, i, hp)),
                  pl.BlockSpec((1, t, LANES), lambda bi, hp, i: (bi, 0, hp)),
                  pl.BlockSpec((1, t, LANES), lambda bi, hp, i: (bi, 0, hp))],
        out_specs=pl.BlockSpec((1, blk, LANES), lambda bi, hp, i: (bi, i, hp)),
        out_shape=jax.ShapeDtypeStruct((b, t, d), F32),
        scratch_shapes=[pltpu.VMEM((t, LANES), BF16),
                        pltpu.VMEM((nb, LANES, blk), BF16),
                        pltpu.VMEM((LANES, LANES), F32),
                        pltpu.VMEM((2, LANES, blk), F32)],
        compiler_params=pltpu.CompilerParams(
            dimension_semantics=("arbitrary", "arbitrary", "arbitrary"),
            vmem_limit_bytes=VMEM_LIMIT),
        name="moba_prompt",
    )(q, k, v)


MEAN_PAGES_PER_STEP = 8
ATTN_BLOCKS_PER_STEP = 2


def _page_means_kernel(pt_ref, *refs):
    del pt_ref
    pages, out_ref = refs[:-1], refs[-1]
    j = pl.program_id(1)
    ppb = MOBA_BLOCK // PAGE_SIZE
    per_step = len(pages) // ppb
    for u in range(per_step):
        tot = jnp.sum(pages[ppb * u][0], axis=0, keepdims=True)
        for w in range(1, ppb):
            tot = tot + jnp.sum(pages[ppb * u + w][0], axis=0, keepdims=True)
        out_ref[0, pl.ds(j * per_step + u, 1), :] = tot * (1.0 / MOBA_BLOCK)


def _page_means(cache_k3, page_table):
    db, n_pages = page_table.shape
    d = cache_k3.shape[2]
    ppb = MOBA_BLOCK // PAGE_SIZE
    nblk = n_pages // ppb
    pps = MEAN_PAGES_PER_STEP
    specs = [pl.BlockSpec((1, PAGE_SIZE, d),
                          functools.partial(lambda b, j, pt, g: (pt[b, j * pps + g], 0, 0), g=g))
             for g in range(pps)]
    grid_spec = pltpu.PrefetchScalarGridSpec(
        num_scalar_prefetch=1,
        grid=(db, n_pages // pps),
        in_specs=specs,
        out_specs=pl.BlockSpec((1, nblk, d), lambda b, j, pt: (b, 0, 0)),
    )
    return pl.pallas_call(
        _page_means_kernel,
        grid_spec=grid_spec,
        out_shape=jax.ShapeDtypeStruct((db, nblk, d), F32),
        compiler_params=pltpu.CompilerParams(dimension_semantics=("arbitrary", "arbitrary"),
                                             vmem_limit_bytes=VMEM_LIMIT),
        name="page_means",
    )(page_table, *([cache_k3] * pps))


def _moba_decode_kernel(pt_ref, q_ref, kn_ref, vn_ref, mean_ref, *refs, n_q, n_heads, k_sel):
    del pt_ref
    bps = ATTN_BLOCKS_PER_STEP
    ppb = MOBA_BLOCK // PAGE_SIZE
    n_pg = bps * ppb
    k_pages, v_pages = refs[:n_pg], refs[n_pg:2 * n_pg]
    o_ref, qbd_scr, selb_scr, m_scr, l_scr, acc_scr = refs[2 * n_pg:]
    j = pl.program_id(1)
    rows = n_q * n_heads
    d = q_ref.shape[2]
    nblk = mean_ref.shape[1]
    scale = HEAD_DIM ** -0.5
    row_id = lax.broadcasted_iota(jnp.int32, (rows, d), 0)
    lane_id = lax.broadcasted_iota(jnp.int32, (rows, d), 1)
    own_head = (lane_id // HEAD_DIM) == (row_id % n_heads)

    @pl.when(j == 0)
    def _():
        qrep = jnp.concatenate(
            [jnp.broadcast_to(q_ref[0, qi:qi + 1, :], (n_heads, d)) for qi in range(n_q)], axis=0)
        qbd = jnp.where(own_head, qrep, 0.0)
        qbd_scr[...] = (qbd * scale).astype(BF16)
        means = jnp.concatenate([mean_ref[0], jnp.zeros((LANES - nblk, d), F32)], axis=0)
        gate = _mm3(_dot_nt, qbd, means)
        blane = lax.broadcasted_iota(jnp.int32, (rows, LANES), 1)
        sel = _top_k_select(gate, blane < nblk, k_sel)
        selb_scr[...] = jnp.where(sel, 0.0, NEG_INF)
        pad = jnp.zeros((LANES - kn_ref.shape[1], d), F32)
        k_new = jnp.concatenate([kn_ref[0], pad], axis=0).astype(BF16)
        v_new = jnp.concatenate([vn_ref[0], pad], axis=0).astype(BF16)
        s_new = _dot_nt(qbd_scr[...], k_new)
        qrow = lax.broadcasted_iota(jnp.int32, (rows, LANES), 0) // n_heads
        s_new = jnp.where(blane <= qrow, s_new, NEG_INF)
        m = jnp.max(s_new, axis=1, keepdims=True)
        pexp = jnp.exp(s_new - m)
        m_scr[...] = m
        l_scr[...] = jnp.sum(pexp, axis=1, keepdims=True)
        acc_scr[...] = _dot_nn(pexp.astype(BF16), v_new)

    qb = qbd_scr[...]
    selb = selb_scr[...]
    blane = lax.broadcasted_iota(jnp.int32, (rows, LANES), 1)
    for u in range(bps):
        n = j * bps + u
        kblk = jnp.concatenate([k_pages[ppb * u + w][0] for w in range(ppb)], axis=0).astype(BF16)
        vblk = jnp.concatenate([v_pages[ppb * u + w][0] for w in range(ppb)], axis=0).astype(BF16)
        bias = jnp.sum(jnp.where(blane == n, selb, 0.0), axis=1, keepdims=True)
        s = _dot_nt(qb, kblk) + bias
        m_old = m_scr[...]
        m_new = jnp.maximum(m_old, jnp.max(s, axis=1, keepdims=True))
        alpha = jnp.exp(m_old - m_new)
        pexp = jnp.exp(s - m_new)
        l_scr[...] = alpha * l_scr[...] + jnp.sum(pexp, axis=1, keepdims=True)
        acc_scr[...] = alpha * acc_scr[...] + _dot_nn(pexp.astype(BF16), vblk)
        m_scr[...] = m_new

    @pl.when(j == pl.num_programs(1) - 1)
    def _():
        out = jnp.where(own_head, acc_scr[...] / l_scr[...], 0.0)
        for qi in range(n_q):
            o_ref[0, qi:qi + 1, :] = jnp.sum(out[qi * n_heads:(qi + 1) * n_heads, :],
                                             axis=0, keepdims=True)


def _moba_decode(q, k_new, v_new, means, cache_k3, cache_v3, page_table):
    db, n_q, d = q.shape
    n_heads = d // HEAD_DIM
    n_pages = page_table.shape[1]
    ppb = MOBA_BLOCK // PAGE_SIZE
    nblk = n_pages // ppb
    bps = ATTN_BLOCKS_PER_STEP
    n_pg = bps * ppb
    k_sel = min(MOBA_TOP_K, nblk)
    pad_rows = 8
    k_new8 = jnp.pad(k_new, ((0, 0), (0, pad_rows - n_q), (0, 0)))
    v_new8 = jnp.pad(v_new, ((0, 0), (0, pad_rows - n_q), (0, 0)))
    page_specs = [pl.BlockSpec((1, PAGE_SIZE, d),
                               functools.partial(lambda b, j, pt, g: (pt[b, j * n_pg + g], 0, 0), g=g))
                  for g in range(n_pg)]
    rows = n_q * n_heads
    grid_spec = pltpu.PrefetchScalarGridSpec(
        num_scalar_prefetch=1,
        grid=(db, nblk // bps),
        in_specs=[pl.BlockSpec((1, n_q, d), lambda b, j, pt: (b, 0, 0)),
                  pl.BlockSpec((1, pad_rows, d), lambda b, j, pt: (b, 0, 0)),
                  pl.BlockSpec((1, pad_rows, d), lambda b, j, pt: (b, 0, 0)),
                  pl.BlockSpec((1, nblk, d), lambda b, j, pt: (b, 0, 0))] + page_specs * 2,
        out_specs=pl.BlockSpec((1, n_q, d), lambda b, j, pt: (b, 0, 0)),
        scratch_shapes=[pltpu.VMEM((rows, d), BF16),
                        pltpu.VMEM((rows, LANES), F32),
                        pltpu.VMEM((rows, 1), F32),
                        pltpu.VMEM((rows, 1), F32),
                        pltpu.VMEM((rows, d), F32)],
    )
    kernel = functools.partial(_moba_decode_kernel, n_q=n_q, n_heads=n_heads, k_sel=k_sel)
    return pl.pallas_call(
        kernel,
        grid_spec=grid_spec,
        out_shape=jax.ShapeDtypeStruct((db, n_q, d), F32),
        compiler_params=pltpu.CompilerParams(dimension_semantics=("arbitrary", "arbitrary"),
                                             vmem_limit_bytes=VMEM_LIMIT),
        name="moba_decode",
    )(page_table, q, k_new8, v_new8, means, *([cache_k3] * n_pg), *([cache_v3] * n_pg))


def _rope_tables(pos):
    half = HEAD_DIM // 2
    inv = jnp.power(ROPE_THETA, -jnp.arange(half, dtype=F32) * (2.0 / HEAD_DIM))
    ang = pos.astype(F32)[:, None] * inv[None, :]
    c, s = jnp.cos(ang), jnp.sin(ang)
    return jnp.concatenate([c, c, c, c], axis=1), jnp.concatenate([-s, s, -s, s], axis=1)


def _pack_state(s):
    b, h, n, _ = s.shape
    sp = s.reshape(b, h // 2, 2, n, n)
    z = jnp.zeros_like(sp[:, :, 0])
    top = jnp.concatenate([sp[:, :, 0], z], axis=-1)
    bot = jnp.concatenate([z, sp[:, :, 1]], axis=-1)
    return jnp.concatenate([top, bot], axis=-2)


def _unpack_state(sbd):
    n = HEAD_DIM
    b, hp = sbd.shape[:2]
    return jnp.stack([sbd[:, :, :n, :n], sbd[:, :, n:, n:]], axis=2).reshape(b, 2 * hp, n, n)


def _trunk(x, pos_tabs, shift0, wkv0, attend, params, *, rows_mode, tm, tm_pre):
    b, t, d = x.shape
    n = b * t
    x2 = x.reshape(n, d)
    p = params
    if rows_mode:
        shift_arg = jnp.zeros((b, t, d), F32).at[:, 0, :].set(shift0).reshape(n, d)
    else:
        shift_arg = shift0.reshape(b, 1, d)
    r, lw, k2, v, kk, kka, g, bonus, last = _rwkv_pre(
        x2, shift_arg, p['rwkv'], seq_len=t, rows_mode=rows_mode, tm=tm_pre)
    if rows_mode:
        new_shift = last.reshape(b, t, d)[:, t - 1, :]
    else:
        new_shift = last.reshape(b, t // tm_pre, d)[:, -1, :]
    cs = SCAN_CHUNK
    t_pad = -(-t // cs) * cs

    def seq(z):
        z = z.reshape(b, t, d)
        return z if t_pad == t else jnp.pad(z, ((0, 0), (0, t_pad - t), (0, 0)))

    o, s_fin = _wkv_scan(seq(r), seq(lw), seq(k2), seq(v), seq(kk), seq(kka), _pack_state(wkv0))
    o2 = o[:, :t, :].reshape(n, d)
    h = _rwkv_post(o2, bonus, g, x2, p['rwkv'], tm=tm)
    h = _ffn(h, p['ffn_norm'][0], p['ffn_wg'][0], p['ffn_wu'][0], p['ffn_wd'][0], p['final_norm'],
             final_norm=False, tm=tm, tf=p['tf'])
    q, k_sh, v_sh = _qkv(h, pos_tabs[0], pos_tabs[1], p['kv_norm'], p['attn_norm'],
                         p['kv_wk'], p['kv_wv'], p['attn_wq'], tm=tm)
    att = attend(q.reshape(b, t, d), k_sh.reshape(b, t, d), v_sh.reshape(b, t, d))
    h = _proj_res(att.reshape(n, d), h, p['attn_wo'], tm=tm)
    y = _ffn(h, p['ffn_norm'][1], p['ffn_wg'][1], p['ffn_wu'][1], p['ffn_wd'][1], p['final_norm'],
             final_norm=True, tm=tm, tf=p['tf'])
    return y.reshape(b, t, d), _unpack_state(s_fin), new_shift, k_sh, v_sh


def kernel(x_prompt, x_sample, state_wkv, state_shift, cache_k, cache_v, page_table, rwkv_norm, rwkv_mu, rwkv_wr, rwkv_wk, rwkv_wv, rwkv_w0, rwkv_w1, rwkv_w2, rwkv_a0, rwkv_a1, rwkv_a2, rwkv_g1, rwkv_g2, rwkv_kk, rwkv_ka, rwkv_rk, rwkv_lnx_w, rwkv_lnx_b, rwkv_wo, kv_norm, kv_wk, kv_wv, attn_norm, attn_wq, attn_wo, ffn_norm, ffn_wg, ffn_wu, ffn_wd, final_norm):
    bp, seq, d = x_prompt.shape
    db, ds, _ = x_sample.shape
    n_heads = d // HEAD_DIM
    f = ffn_wg.shape[-1]
    assert rwkv_norm.shape[0] == 1 and attn_norm.shape[0] == 1, "one RWKV layer then one MoBA layer"
    row = lambda a: a.reshape(1, d)
    head_of_lane = jnp.arange(d) // HEAD_DIM
    e = (head_of_lane[:, None] == jnp.arange(LANES)[None, :]).astype(BF16)
    rwkv = {
        'norm': row(rwkv_norm[0]), 'mu': rwkv_mu[0],
        'wr': rwkv_wr[0].astype(BF16), 'wk': rwkv_wk[0].astype(BF16), 'wv': rwkv_wv[0].astype(BF16),
        'w0': row(rwkv_w0[0]), 'w1': rwkv_w1[0], 'w2': rwkv_w2[0],
        'a0': row(rwkv_a0[0]), 'a1': rwkv_a1[0], 'a2': rwkv_a2[0],
        'g1': rwkv_g1[0], 'g2': rwkv_g2[0],
        'kk': row(rwkv_kk[0]), 'ka': row(rwkv_ka[0]), 'rk': row(rwkv_rk[0]),
        'lnx_w': row(rwkv_lnx_w[0]), 'lnx_b': row(rwkv_lnx_b[0]),
        'wo': rwkv_wo[0].astype(BF16), 'e': e, 'et': e.T,
    }
    tf = f // 2 if (f // 2) % LANES == 0 else f
    params = {
        'rwkv': rwkv,
        'ffn_norm': ffn_norm.reshape(-1, 1, d),
        'ffn_wg': ffn_wg.astype(BF16), 'ffn_wu': ffn_wu.astype(BF16), 'ffn_wd': ffn_wd.astype(BF16),
        'final_norm': row(final_norm), 'tf': tf,
        'kv_norm': row(kv_norm), 'attn_norm': row(attn_norm[0]),
        'kv_wk': kv_wk.astype(BF16), 'kv_wv': kv_wv.astype(BF16),
        'attn_wq': attn_wq[0].astype(BF16), 'attn_wo': attn_wo[0].astype(BF16),
    }

    tabs_p = _rope_tables(jnp.arange(seq, dtype=jnp.int32))
    tm_p = min(512, seq)
    y_p, wkv_p, shift_p, k_p, v_p = _trunk(
        x_prompt, tabs_p, jnp.zeros((bp, d), F32), jnp.zeros((bp, n_heads, HEAD_DIM, HEAD_DIM), F32),
        _moba_prompt, params, rows_mode=False, tm=tm_p, tm_pre=min(256, seq))
    pages_shape = (bp, seq // PAGE_SIZE, PAGE_SIZE, n_heads, HEAD_DIM)

    n_pages = page_table.shape[1]
    past = n_pages * PAGE_SIZE
    assert past % MOBA_BLOCK == 0, "decode rows start a fresh MoBA block"
    assert ds <= 8 and (db * ds) % 8 == 0
    pos_s = past + jnp.arange(ds, dtype=jnp.int32)
    tabs_s = tuple(jnp.tile(tb, (db, 1)) for tb in _rope_tables(pos_s))
    cache_k3 = cache_k.reshape(cache_k.shape[0], PAGE_SIZE, d)
    cache_v3 = cache_v.reshape(cache_v.shape[0], PAGE_SIZE, d)
    means = _page_means(cache_k3, page_table)
    attend_s = lambda q, k, v: _moba_decode(q, k, v, means, cache_k3, cache_v3, page_table)
    y_s, wkv_s, shift_s, k_s, v_s = _trunk(
        x_sample, tabs_s, state_shift[0], state_wkv[0], attend_s, params,
        rows_mode=True, tm=db * ds, tm_pre=db * ds)

    return (y_p, y_s, wkv_p[None], shift_p[None],
            k_p.reshape(pages_shape), v_p.reshape(pages_shape),
            wkv_s[None], shift_s[None],
            k_s.reshape(db, ds, n_heads, HEAD_DIM), v_s.reshape(db, ds, n_heads, HEAD_DIM))
```
